```python
import math
import jax
import jax.numpy as jnp
from jax import lax
import numpy as np

D_MODEL = 1024
BATCH = 8
SEQ = 4096
DEPTH = 2

N_EVEN = (DEPTH + 1) // 2
N_ODD = DEPTH // 2
NORM_EPS = 1e-6

MLSTM_HEADS = 4
MLSTM_HD = D_MODEL // 8
MLSTM_W = MLSTM_HEADS * MLSTM_HD
MLSTM_CHUNK = 128
CONV_W = 5
POOL_WINDOWS = (2, 4, 8, 16)
POOL_GROUPS = len(POOL_WINDOWS)
POOL_GC = D_MODEL // 8
POOL_W = POOL_GROUPS * POOL_GC
MIX_W = MLSTM_W + POOL_W
N_GATE = 2 * 2 * MLSTM_HEADS
EVEN_IN = 4 * MLSTM_W + POOL_W + N_GATE
ATT_HD = 128
ATT_HEADS = D_MODEL // ATT_HD
ATT_KV_HEADS = ATT_HEADS // 4
ATT_GROUP = ATT_HEADS // ATT_KV_HEADS
ATT_W = ATT_HEADS * ATT_HD
ODD_IN = (ATT_HEADS + 2 * ATT_KV_HEADS) * ATT_HD
Q_BLOCK = 128
ROPE_THETA = 10000.0
GRID_W = 64
PEER_HEADS = 8
PEER_NKEYS = 128
PEER_EXPERTS = PEER_NKEYS * PEER_NKEYS
PEER_QDIM = 256
PEER_TOPK = 16
PEER_CHUNK = 128

kernel_name = 'hybrid_mlstm_pool_axialgqa_peer_encoder'

f32 = jnp.float32


def _unit_rms(x):
    xf = x.astype(f32)
    return xf * lax.rsqrt(jnp.mean(xf * xf, axis=-1, keepdims=True) + NORM_EPS)


def _rms_norm(x, g):
    return (_unit_rms(x) * g.astype(f32)).astype(x.dtype)


def _dwconv_centred(a, w, b):
    y = lax.conv_general_dilated(a, w.astype(a.dtype), window_strides=(1,), padding='SAME',
                                 dimension_numbers=('NWC', 'WIO', 'NWC'),
                                 feature_group_count=a.shape[-1])
    return y + b


def _mlstm_bidir_scan(q, k, v, log_i, log_f):
    lead = q.shape[:3]
    S = q.shape[3]
    L = MLSTM_CHUNK
    nc = S // L

    def to_chunks(a):
        a = a.reshape(lead + (nc, L) + a.shape[4:])
        return jnp.moveaxis(a, 3, 0)

    in_chunk_mask = jnp.tril(jnp.ones((L, L), dtype=bool))

    def step(carry, inp):
        C, n, m = carry
        qc, kc, vc, li, lf = inp
        b = jnp.cumsum(lf, axis=-1)
        b_end = b[..., -1]
        log_w = jnp.where(in_chunk_mask, b[..., :, None] - b[..., None, :] + li[..., None, :], -jnp.inf)
        log_carry = b + m[..., None]
        m_t = jnp.maximum(log_carry, jnp.max(log_w, axis=-1))
        w_intra = jnp.exp(log_w - m_t[..., None])
        w_carry = jnp.exp(log_carry - m_t)
        sc = jnp.einsum('...jd,...sd->...js', qc, kc) * w_intra
        num = jnp.einsum('...js,...sv->...jv', sc, vc) + w_carry[..., None] * jnp.einsum('...jd,...dv->...jv', qc, C)
        den = jnp.sum(sc, axis=-1) + w_carry * jnp.einsum('...jd,...d->...j', qc, n)
        h = num / jnp.maximum(jnp.abs(den), jnp.exp(-m_t))[..., None]
        log_end = b_end[..., None] - b + li
        m_new = jnp.maximum(b_end + m, jnp.max(log_end, axis=-1))
        w_end = jnp.exp(log_end - m_new[..., None])
        decay = jnp.exp(b_end + m - m_new)
        kw = kc * w_end[..., None]
        C_new = decay[..., None, None] * C + jnp.einsum('...sd,...sv->...dv', kw, vc)
        n_new = decay[..., None] * n + jnp.sum(kw, axis=-2)
        return (C_new, n_new, m_new), h

    d = q.shape[-1]
    dv = v.shape[-1]
    init = (jnp.zeros(lead + (d, dv), f32), jnp.zeros(lead + (d,), f32), jnp.zeros(lead, f32))
    _, hs = lax.scan(step, init, (to_chunks(q), to_chunks(k), to_chunks(v), to_chunks(log_i), to_chunks(log_f)))
    return jnp.moveaxis(hs, 0, 3).reshape(lead + (S, dv))


def _multiscale_pool(p, pool_w, pool_scale):
    B, S, _ = p.shape
    pf = p.astype(f32)
    csum = jnp.concatenate([jnp.zeros((B, 1, POOL_W), f32), jnp.cumsum(pf, axis=1)], axis=1)
    t = jnp.arange(S)
    outs = []
    for gi, win in enumerate(POOL_WINDOWS):
        lo = jnp.clip(t - win // 2, 0, S)
        hi = jnp.clip(t + win // 2, 0, S)
        sl = slice(gi * POOL_GC, (gi + 1) * POOL_GC)
        cg = csum[..., sl]
        mean = (cg[:, hi] - cg[:, lo]) / (hi - lo).astype(f32)[None, :, None]
        y = (mean - pf[..., sl]).astype(p.dtype)
        outs.append(jnp.einsum('bsc,ce->bse', y, pool_w[gi]))
    return jnp.concatenate(outs, axis=-1) * pool_scale


def _mlstm_pool_mixer(h, w_in, b_in, conv_w, conv_b, mnorm_g, pool_w, pool_scale, w_out):
    B, S, _ = h.shape
    z = jnp.einsum('bsd,de->bse', h, w_in) + b_in
    qk, v, o, p, g = jnp.split(z, [2 * MLSTM_W, 3 * MLSTM_W, 4 * MLSTM_W, 4 * MLSTM_W + POOL_W], axis=-1)
    qk = jax.nn.silu(_dwconv_centred(qk, conv_w, conv_b))
    q, k = jnp.split(qk, 2, axis=-1)

    def heads(a):
        return a.reshape(B, S, MLSTM_HEADS, MLSTM_HD).transpose(0, 2, 1, 3).astype(f32)

    q, k, v = heads(q), heads(k) * (MLSTM_HD ** -0.5), heads(v)

    def both(a):
        return jnp.stack([a, jnp.flip(a, axis=2)])

    g = g.astype(f32).reshape(B, S, 2, 2, MLSTM_HEADS).transpose(2, 3, 0, 4, 1)
    log_i = jnp.stack([g[0, 0], jnp.flip(g[0, 1], axis=-1)])
    log_f = jax.nn.log_sigmoid(jnp.stack([g[1, 0], jnp.flip(g[1, 1], axis=-1)]))
    hs = _mlstm_bidir_scan(both(q), both(k), both(v), log_i, log_f)
    hm = (hs[0] + jnp.flip(hs[1], axis=2)).transpose(0, 2, 1, 3)
    hm = _rms_norm(hm, mnorm_g.reshape(MLSTM_HEADS, MLSTM_HD)).astype(h.dtype)
    hm = hm.reshape(B, S, MLSTM_W) * jax.nn.sigmoid(o)
    yp = _multiscale_pool(p, pool_w, pool_scale)
    return jnp.einsum('bse,ed->bsd', jnp.concatenate([hm, yp], axis=-1), w_out)


def _axial_rope_tables(S):
    rows = S // GRID_W
    r, cidx = jnp.meshgrid(jnp.arange(rows), jnp.arange(GRID_W), indexing='ij')
    axis_dim = ATT_HD // 2
    freqs = ROPE_THETA ** (-jnp.arange(0, axis_dim, 2, dtype=f32) / axis_dim)
    ang = jnp.concatenate([r.reshape(-1, 1).astype(f32) * freqs,
                           cidx.reshape(-1, 1).astype(f32) * freqs], axis=-1)
    return jnp.cos(ang), jnp.sin(ang)


def _apply_rope(x, cos, sin):
    xr = x.astype(f32).reshape(x.shape[:-1] + (ATT_HD // 2, 2))
    x0, x1 = xr[..., 0], xr[..., 1]
    c = cos[None, :, None, :]
    s = sin[None, :, None, :]
    out = jnp.stack([x0 * c - x1 * s, x0 * s + x1 * c], axis=-1).reshape(x.shape)
    return out.astype(x.dtype)


def _gqa_axial_mixer(h, w_in, qn_g, kn_g, w_out):
    B, S, _ = h.shape
    z = jnp.einsum('bsd,de->bse', h, w_in)
    q, k, v = jnp.split(z, [ATT_W, ATT_W + ATT_KV_HEADS * ATT_HD], axis=-1)
    q = _rms_norm(q.reshape(B, S, ATT_HEADS, ATT_HD), qn_g)
    k = _rms_norm(k.reshape(B, S, ATT_KV_HEADS, ATT_HD), kn_g)
    v = v.reshape(B, S, ATT_KV_HEADS, ATT_HD)
    cos, sin = _axial_rope_tables(S)
    q = _apply_rope(q, cos, sin)
    k = _apply_rope(k, cos, sin)
    qb = jnp.moveaxis(q.reshape(B, S // Q_BLOCK, Q_BLOCK, ATT_KV_HEADS, ATT_GROUP, ATT_HD), 1, 0)
    scale = ATT_HD ** -0.5

    def attend(q_blk):
        s = jnp.einsum('bqkgd,bskd->bkgqs', q_blk, k).astype(f32) * scale
        pr = jax.nn.softmax(s, axis=-1).astype(v.dtype)
        return jnp.einsum('bkgqs,bskd->bqkgd', pr, v)

    out = lax.map(attend, qb)
    out = jnp.moveaxis(out, 0, 1).reshape(B, S, ATT_W)
    return jnp.einsum('bse,ed->bsd', out, w_out)


def _peer(h, w_q, sub_keys, u, v):
    B, S, D = h.shape
    xt = h.reshape(B * S // PEER_CHUNK, PEER_CHUNK, D)
    half = PEER_QDIM // 2
    nk2 = PEER_TOPK * PEER_TOPK

    def chunk(xc):
        q = jnp.einsum('td,de->te', xc, w_q).reshape(PEER_CHUNK, PEER_HEADS, 2, half)
        q = _unit_rms(q)
        s = jnp.einsum('thpc,pnc->thpn', q, sub_keys.astype(f32))
        s1, i1 = lax.top_k(s[:, :, 0], PEER_TOPK)
        s2, i2 = lax.top_k(s[:, :, 1], PEER_TOPK)
        cand_s = (s1[..., :, None] + s2[..., None, :]).reshape(PEER_CHUNK, PEER_HEADS, nk2)
        cand_i = (i1[..., :, None] * PEER_NKEYS + i2[..., None, :]).reshape(PEER_CHUNK, PEER_HEADS, nk2)
        top_s, pos = lax.top_k(cand_s, PEER_TOPK)
        idx = jnp.take_along_axis(cand_i, pos, axis=-1)
        gate = jax.nn.softmax(top_s, axis=-1)
        act = jax.nn.gelu(jnp.einsum('thkd,td->thk', u[idx], xc).astype(f32), approximate=False)
        coef = (gate * act).astype(xc.dtype)
        return jnp.einsum('thk,thkd->td', coef, v[idx])

    return lax.map(chunk, xt).reshape(B, S, D)


def setup_inputs(seed: int = 0) -> dict:
    key = jax.random.key(seed)
    ks = jax.random.split(key, 28)
    D = D_MODEL

    def nrm(k, shape, s):
        return jax.random.normal(k, shape, f32) * s

    ev_b_in = nrm(ks[7], (N_EVEN, EVEN_IN), 0.02)
    f0 = 4 * MLSTM_W + POOL_W + 2 * MLSTM_HEADS
    ev_b_in = ev_b_in.at[:, f0:f0 + 2 * MLSTM_HEADS].add(
        jax.random.uniform(ks[8], (N_EVEN, 2 * MLSTM_HEADS), f32, 3.0, 6.0))
    return {
        'x': nrm(ks[0], (BATCH, SEQ, D), 1.0),
        'c': nrm(ks[1], (BATCH, D), 1.0),
        'ada_w': nrm(ks[2], (DEPTH, D, 6 * D), 0.5 * D ** -0.5),
        'ada_b': nrm(ks[3], (DEPTH, 6 * D), 0.02),
        'norm_mix_g': 1.0 + nrm(ks[4], (DEPTH, D), 0.05),
        'norm_ffn_g': 1.0 + nrm(ks[5], (DEPTH, D), 0.05),
        'ev_w_in': nrm(ks[6], (N_EVEN, D, EVEN_IN), D ** -0.5),
        'ev_b_in': ev_b_in,
        'ev_conv_w': nrm(ks[9], (N_EVEN, CONV_W, 1, 2 * MLSTM_W), CONV_W ** -0.5),
        'ev_conv_b': nrm(ks[10], (N_EVEN, 2 * MLSTM_W), 0.02),
        'ev_mnorm_g': 1.0 + nrm(ks[11], (N_EVEN, MLSTM_W), 0.05),
        'ev_pool_w': nrm(ks[12], (N_EVEN, POOL_GROUPS, POOL_GC, POOL_GC), POOL_GC ** -0.5),
        'ev_pool_scale': 1.0 + nrm(ks[13], (N_EVEN, POOL_W), 0.1),
        'ev_w_out': nrm(ks[14], (N_EVEN, MIX_W, D), MIX_W ** -0.5),
        'od_w_in': nrm(ks[15], (N_ODD, D, ODD_IN), D ** -0.5),
        'od_qnorm_g': 1.0 + nrm(ks[16], (N_ODD, ATT_HD), 0.05),
        'od_knorm_g': 1.0 + nrm(ks[17], (N_ODD, ATT_HD), 0.05),
        'od_w_out': nrm(ks[18], (N_ODD, ATT_W, D), ATT_W ** -0.5),
        'peer_w_q': nrm(ks[19], (DEPTH, D, PEER_HEADS * PEER_QDIM), D ** -0.5),
        'peer_keys': nrm(ks[20], (DEPTH, 2, PEER_NKEYS, PEER_QDIM // 2), (PEER_QDIM // 2) ** -0.5),
        'peer_u': nrm(ks[21], (DEPTH, PEER_EXPERTS, D), D ** -0.5),
        'peer_v': nrm(ks[22], (DEPTH, PEER_EXPERTS, D), 0.5),
        'final_g': 1.0 + nrm(ks[23], (D,), 0.05),
    }


def reference(x, c, ada_w, ada_b, norm_mix_g, norm_ffn_g, ev_w_in, ev_b_in, ev_conv_w, ev_conv_b,
              ev_mnorm_g, ev_pool_w, ev_pool_scale, ev_w_out, od_w_in, od_qnorm_g, od_knorm_g, od_w_out,
              peer_w_q, peer_keys, peer_u, peer_v, final_g):
    cond = jax.nn.silu(c)
    for i in range(DEPTH):
        mod = jnp.einsum('bd,de->be', cond, ada_w[i]) + ada_b[i]
        sh1, sc1, g1, sh2, sc2, g2 = [m[:, None, :] for m in jnp.split(mod, 6, axis=-1)]
        hmix = _rms_norm(x, norm_mix_g[i]) * (1 + sc1) + sh1
        if i % 2 == 0:
            j = i // 2
            y = _mlstm_pool_mixer(hmix, ev_w_in[j], ev_b_in[j], ev_conv_w[j], ev_conv_b[j], ev_mnorm_g[j],
                                  ev_pool_w[j], ev_pool_scale[j], ev_w_out[j])
        else:
            j = i // 2
            y = _gqa_axial_mixer(hmix, od_w_in[j], od_qnorm_g[j], od_knorm_g[j], od_w_out[j])
        x = x + g1 * y
        hffn = _rms_norm(x, norm_ffn_g[i]) * (1 + sc2) + sh2
        x = x + g2 * _peer(hffn, peer_w_q[i], peer_keys[i], peer_u[i], peer_v[i])
    return _rms_norm(x, final_g)
```

```python
import functools
import math

import jax
import jax.numpy as jnp
from jax import lax
from jax.experimental import pallas as pl
from jax.experimental.pallas import tpu as pltpu

f32 = jnp.float32
bf16 = jnp.bfloat16

NORM_EPS = 1e-6
LANES = 128
VMEM_LIMIT_BYTES = 56 * 1024 * 1024

MLSTM_HEADS = 4
MLSTM_HD = 128
MLSTM_W = MLSTM_HEADS * MLSTM_HD
MLSTM_CHUNK = 128
CONV_W = 5
POOL_WINDOWS = (2, 4, 8, 16)
POOL_GC = 128
POOL_W = len(POOL_WINDOWS) * POOL_GC
POOL_PAD = 8
N_GATE = 16
ATT_HD = 128
ATT_KV_HEADS = 2
ATT_GROUP = 4
ATT_HEADS = ATT_KV_HEADS * ATT_GROUP
ROPE_THETA = 10000.0
GRID_W = 64
PEER_HEADS = 8
PEER_NKEYS = 128
PEER_HALF = 128
PEER_TOPK = 16
NOT_SELECTED = 64.0


def _cparams(*sem):
    return pltpu.CompilerParams(dimension_semantics=sem, vmem_limit_bytes=VMEM_LIMIT_BYTES)


def _nt_dot(a, b):
    return lax.dot_general(a, b, (((1,), (1,)), ((), ())), preferred_element_type=f32)


def _modulated_norm(x, gain, scale, shift):
    ms = jnp.mean(x * x, axis=-1, keepdims=True)
    h = x * lax.rsqrt(ms + NORM_EPS) * gain
    return h * (1.0 + scale) + shift


def _ada_kernel(c_ref, w_ref, b_ref, o_ref):
    c = c_ref[...]
    cond = c * jax.nn.sigmoid(c)
    o_ref[0] = jnp.dot(cond, w_ref[0], preferred_element_type=f32,
                       precision=lax.Precision.HIGHEST) + b_ref[0]


def _ada_mod(c, ada_w, ada_b):
    depth, d, n = ada_w.shape
    bsz = c.shape[0]
    tn = d
    return pl.pallas_call(
        _ada_kernel,
        grid=(depth, n // tn),
        in_specs=[pl.BlockSpec((bsz, d), lambda i, j: (0, 0)),
                  pl.BlockSpec((1, d, tn), lambda i, j: (i, 0, j)),
                  pl.BlockSpec((1, 1, tn), lambda i, j: (i, 0, j))],
        out_specs=pl.BlockSpec((1, bsz, tn), lambda i, j: (i, 0, j)),
        out_shape=jax.ShapeDtypeStruct((depth, bsz, n), f32),
        compiler_params=_cparams("arbitrary", "arbitrary"),
        name="ada_mod",
    )(c, ada_w, ada_b.reshape(depth, 1, n))


def _even_inproj_kernel(x_ref, mod_ref, g_ref, wn_ref, bn_ref, wkt_ref, bkt_ref, wgt_ref, bgt_ref,
                        q_ref, v_ref, o_ref, p_ref, kt_ref, gt_ref):
    mod = mod_ref[0]
    h = _modulated_norm(x_ref[...], g_ref[...], mod[1:2], mod[0:1]).astype(bf16)
    z = jnp.dot(h, wn_ref[...], preferred_element_type=f32) + bn_ref[...]
    w = MLSTM_W
    q_ref[...] = z[:, 0:w]
    v_ref[...] = z[:, w:2 * w]
    o_ref[...] = z[:, 2 * w:3 * w]
    p_ref[...] = z[:, 3 * w:4 * w]
    kt_ref[0] = _nt_dot(wkt_ref[...], h) + bkt_ref[...]
    gt_ref[0] = _nt_dot(wgt_ref[...], h) + bgt_ref[...]


def _even_inproj(x2, mod, gain, w_in, b_in, seq, tm):
    t, d = x2.shape
    w = MLSTM_W
    nb = t // seq
    wn = jnp.concatenate([w_in[:, 0:w], w_in[:, 2 * w:5 * w]], axis=1).astype(bf16)
    bn = jnp.concatenate([b_in[0:w], b_in[2 * w:5 * w]]).reshape(1, 4 * w)
    wkt = w_in[:, w:2 * w].T.astype(bf16)
    bkt = b_in[w:2 * w].reshape(w, 1)
    wgt = w_in[:, 5 * w:].T.astype(bf16)
    bgt = b_in[5 * w:].reshape(N_GATE, 1)
    tps = seq // tm
    tok = lambda i: (i, 0)
    const = lambda i: (0, 0)
    outs = pl.pallas_call(
        _even_inproj_kernel,
        grid=(t // tm,),
        in_specs=[pl.BlockSpec((tm, d), tok),
                  pl.BlockSpec((1, 6, d), lambda i: (i // tps, 0, 0)),
                  pl.BlockSpec((1, d), const),
                  pl.BlockSpec((d, 4 * w), const),
                  pl.BlockSpec((1, 4 * w), const),
                  pl.BlockSpec((w, d), const),
                  pl.BlockSpec((w, 1), const),
                  pl.BlockSpec((N_GATE, d), const),
                  pl.BlockSpec((N_GATE, 1), const)],
        out_specs=[pl.BlockSpec((tm, w), tok)] * 4 + [
            pl.BlockSpec((1, w, tm), lambda i: (i // tps, 0, i % tps)),
            pl.BlockSpec((1, N_GATE, tm), lambda i: (i // tps, 0, i % tps))],
        out_shape=[jax.ShapeDtypeStruct((t, w), f32)] * 4 + [
            jax.ShapeDtypeStruct((nb, w, seq), f32),
            jax.ShapeDtypeStruct((nb, N_GATE, seq), f32)],
        compiler_params=_cparams("arbitrary"),
        name="even_inproj",
    )(x2, mod, gain.reshape(1, d), wn, bn, wkt, bkt, wgt, bgt)
    return outs


def _log_sigmoid(x):
    return jnp.minimum(x, 0.0) - jnp.log1p(jnp.exp(-jnp.abs(x)))


def _lane_scan_max(a, reverse):
    n = a.shape[-1]
    lane = lax.broadcasted_iota(jnp.int32, a.shape, 1)
    k = 1
    while k < n:
        if reverse:
            shifted = pltpu.roll(a, n - k, axis=1)
            a = jnp.maximum(a, jnp.where(lane < n - k, shifted, -jnp.inf))
        else:
            shifted = pltpu.roll(a, k, axis=1)
            a = jnp.maximum(a, jnp.where(lane >= k, shifted, -jnp.inf))
        k *= 2
    return a


def _gate_tables(li, lfp, reverse, rows_ref, a_ref, wend_ref, dec_ref, slot):
    nc, L = li.shape
    lf = _log_sigmoid(lfp)
    r = lax.broadcasted_iota(jnp.int32, (L, L), 0)
    cidx = lax.broadcasted_iota(jnp.int32, (L, L), 1)
    tri = (r >= cidx) if reverse else (r <= cidx)
    b = jnp.dot(lf, tri.astype(f32), preferred_element_type=f32, precision=lax.Precision.HIGHEST)
    a = li - b
    amax = _lane_scan_max(a, reverse)
    end = 0 if reverse else L - 1
    b_end = jnp.broadcast_to(b[:, end:end + 1], (nc, L))
    a_end = jnp.broadcast_to(amax[:, end:end + 1], (nc, L))
    m = jnp.zeros((1, L), f32)
    m_rows = [None] * nc
    mn_rows = [None] * nc
    order = range(nc - 1, -1, -1) if reverse else range(nc)
    for c in order:
        m_rows[c] = m
        m = b_end[c:c + 1] + jnp.maximum(m, a_end[c:c + 1])
        mn_rows[c] = m
    m_all = jnp.concatenate(m_rows, axis=0)
    m_next = jnp.concatenate(mn_rows, axis=0)
    u = -jnp.maximum(m_all, amax)
    a_ref[slot] = a
    wend_ref[slot] = jnp.exp(a + b_end - m_next)
    dec_ref[slot] = jnp.exp(b_end + m_all - m_next)
    wc = jnp.exp(m_all + u)
    en = jnp.exp(u - b)
    for c in range(nc):
        base = c * 8 + slot * 3
        rows_ref[base:base + 1, :] = u[c:c + 1]
        rows_ref[base + 1:base + 2, :] = wc[c:c + 1]
        rows_ref[base + 2:base + 3, :] = en[c:c + 1]


def _mlstm_kernel(q_ref, kt_ref, v_ref, o_ref, g_ref, cwq_ref, cbq_ref, cwk_ref, cbk_ref, mg_ref,
                  out_ref, q_scr, kt_scr, va_scr, h_scr, rows_scr, col_scr, a_scr, wend_scr,
                  dec_scr, c_scr):
    S = q_ref.shape[1]
    L = MLSTM_CHUNK
    nc = S // L
    hd = MLSTM_HD
    head = pl.program_id(1)

    qp = q_ref[0]
    row = lax.broadcasted_iota(jnp.int32, (S, hd), 0)
    acc = qp * cwq_ref[2:3, :]
    for wi in range(CONV_W):
        off = wi - CONV_W // 2
        if off == 0:
            continue
        sh = pltpu.roll(qp, (-off) % S, axis=0)
        ok = (row + off >= 0) & (row + off < S)
        acc = acc + jnp.where(ok, sh, 0.0) * cwq_ref[wi:wi + 1, :]
    acc = acc + cbq_ref[...]
    q_scr[...] = (acc * jax.nn.sigmoid(acc)).astype(bf16)

    kp = kt_ref[0]
    col = lax.broadcasted_iota(jnp.int32, (hd, S), 1)
    acc = kp * cwk_ref[:, 2:3]
    for wi in range(CONV_W):
        off = wi - CONV_W // 2
        if off == 0:
            continue
        sh = pltpu.roll(kp, (-off) % S, axis=1)
        ok = (col + off >= 0) & (col + off < S)
        acc = acc + jnp.where(ok, sh, 0.0) * cwk_ref[:, wi:wi + 1]
    acc = acc + cbk_ref[...]
    kt = acc * jax.nn.sigmoid(acc) * (hd ** -0.5)
    for c in range(nc):
        kt_scr[c] = kt[:, c * L:(c + 1) * L]

    lane2 = lax.broadcasted_iota(jnp.int32, (S, hd), 1)
    va_scr[:, 0:hd] = v_ref[0].astype(bf16)
    va_scr[:, hd:2 * hd] = jnp.where(lane2 == 0, 1.0, 0.0).astype(bf16)

    rows_scr[...] = jnp.zeros_like(rows_scr)
    for d in range(2):
        li = g_ref[0, d * MLSTM_HEADS + head]
        lfp = g_ref[0, 2 * MLSTM_HEADS + d * MLSTM_HEADS + head]
        _gate_tables(li, lfp, d == 1, rows_scr, a_scr, wend_scr, dec_scr, d)
    cols = rows_scr[...].T
    for c in range(nc):
        col_scr[c] = cols[:, c * 8:(c + 1) * 8]

    c_scr[...] = jnp.zeros_like(c_scr)
    rr = lax.broadcasted_iota(jnp.int32, (L, L), 0)
    cc = lax.broadcasted_iota(jnp.int32, (L, L), 1)

    def chunk_step(c, d):
        start = pl.multiple_of(c * L, L)
        qc = q_scr[pl.ds(start, L), :]
        ktc = kt_scr[c]
        vac = va_scr[pl.ds(start, L), :]
        colv = col_scr[c]
        u = colv[:, 3 * d:3 * d + 1]
        wcar = colv[:, 3 * d + 1:3 * d + 2]
        en = colv[:, 3 * d + 2:3 * d + 3]
        a_row = a_scr[d, pl.ds(c, 1), :]
        wend_row = wend_scr[d, pl.ds(c, 1), :]
        dec_row = dec_scr[d, pl.ds(c, 1), :]
        mask = (cc >= rr) if d == 1 else (cc <= rr)
        s = jnp.dot(qc, ktc.astype(bf16), preferred_element_type=f32)
        w = jnp.exp(jnp.where(mask, u + a_row, -jnp.inf))
        sc = (s * w).astype(bf16)
        cst = c_scr[d]
        nd = jnp.dot(sc, vac, preferred_element_type=f32) + wcar * jnp.dot(
            qc, cst.astype(bf16), preferred_element_type=f32)
        den = jnp.maximum(jnp.abs(nd[:, hd:hd + 1]), en)
        h_scr[d, pl.ds(start, L), :] = nd[:, 0:hd] / den
        kw = (ktc * wend_row).astype(bf16)
        dec2 = jnp.concatenate([dec_row, dec_row], axis=1)
        c_scr[d] = dec2 * cst + jnp.dot(kw, vac, preferred_element_type=f32)

    def body(i, carry):
        chunk_step(i, 0)
        chunk_step(nc - 1 - i, 1)
        return carry

    lax.fori_loop(0, nc, body, 0)

    hsum = h_scr[0] + h_scr[1]
    ms = jnp.mean(hsum * hsum, axis=-1, keepdims=True)
    hn = hsum * lax.rsqrt(ms + NORM_EPS) * mg_ref[...]
    out_ref[0] = hn * jax.nn.sigmoid(o_ref[0])


def _mlstm(q, kt, v, o, gt, conv_w, conv_b, mnorm_g, nb, seq):
    w, hd, L = MLSTM_W, MLSTM_HD, MLSTM_CHUNK
    nc = seq // L
    q3 = q.reshape(nb, seq, w)
    v3 = v.reshape(nb, seq, w)
    o3 = o.reshape(nb, seq, w)
    g4 = gt.reshape(nb, N_GATE, nc, L)
    cw = conv_w.reshape(CONV_W, 2 * w)
    cwq = cw[:, :w]
    cwk = cw[:, w:].T
    cbq = conv_b[:w].reshape(1, w)
    cbk = conv_b[w:].reshape(w, 1)
    tokblk = pl.BlockSpec((1, seq, hd), lambda b, h: (b, 0, h))
    return pl.pallas_call(
        _mlstm_kernel,
        grid=(nb, MLSTM_HEADS),
        in_specs=[tokblk,
                  pl.BlockSpec((1, hd, seq), lambda b, h: (b, h, 0)),
                  tokblk, tokblk,
                  pl.BlockSpec((1, N_GATE, nc, L), lambda b, h: (b, 0, 0, 0)),
                  pl.BlockSpec((CONV_W, hd), lambda b, h: (0, h)),
                  pl.BlockSpec((1, hd), lambda b, h: (0, h)),
                  pl.BlockSpec((hd, CONV_W), lambda b, h: (h, 0)),
                  pl.BlockSpec((hd, 1), lambda b, h: (h, 0)),
                  pl.BlockSpec((1, hd), lambda b, h: (0, h))],
        out_specs=tokblk,
        out_shape=jax.ShapeDtypeStruct((nb, seq, w), f32),
        scratch_shapes=[pltpu.VMEM((seq, hd), bf16),
                        pltpu.VMEM((nc, hd, L), f32),
                        pltpu.VMEM((seq, 2 * hd), bf16),
                        pltpu.VMEM((2, seq, hd), f32),
                        pltpu.VMEM((nc * 8, L), f32),
                        pltpu.VMEM((nc, L, 8), f32),
                        pltpu.VMEM((2, nc, L), f32),
                        pltpu.VMEM((2, nc, L), f32),
                        pltpu.VMEM((2, nc, L), f32),
                        pltpu.VMEM((2, hd, 2 * hd), f32)],
        compiler_params=_cparams("arbitrary", "arbitrary"),
        name="mlstm",
    )(q3, kt, v3, o3, g4, cwq, cbq, cwk, cbk, mnorm_g.reshape(1, w))


def _pool_kernel(p_ref, w_ref, sc_ref, out_ref, pad_scr):
    S = p_ref.shape[1]
    pad_scr[0:POOL_PAD, :] = jnp.zeros((POOL_PAD, POOL_W), f32)
    pad_scr[POOL_PAD + S:POOL_PAD + S + POOL_PAD, :] = jnp.zeros((POOL_PAD, POOL_W), f32)
    pad_scr[POOL_PAD:POOL_PAD + S, :] = p_ref[0]
    t = lax.broadcasted_iota(jnp.int32, (S, POOL_GC), 0)
    for gi, win in enumerate(POOL_WINDOWS):
        sl = slice(gi * POOL_GC, (gi + 1) * POOL_GC)
        half = win // 2
        tot = pad_scr[POOL_PAD - half:POOL_PAD - half + S, sl]
        for off in range(-half + 1, half):
            tot = tot + pad_scr[POOL_PAD + off:POOL_PAD + off + S, sl]
        cnt = (jnp.minimum(t + half, S) - jnp.maximum(t - half, 0)).astype(f32)
        y = (tot / cnt - p_ref[0, :, sl]).astype(bf16)
        out_ref[0, :, sl] = jnp.dot(y, w_ref[gi].astype(bf16), preferred_element_type=f32) * sc_ref[:, sl]


def _pool(p, pool_w, pool_scale, nb, seq):
    p3 = p.reshape(nb, seq, POOL_W)
    return pl.pallas_call(
        _pool_kernel,
        grid=(nb,),
        in_specs=[pl.BlockSpec((1, seq, POOL_W), lambda b: (b, 0, 0)),
                  pl.BlockSpec(pool_w.shape, lambda b: (0, 0, 0)),
                  pl.BlockSpec((1, POOL_W), lambda b: (0, 0))],
        out_specs=pl.BlockSpec((1, seq, POOL_W), lambda b: (b, 0, 0)),
        out_shape=jax.ShapeDtypeStruct((nb, seq, POOL_W), f32),
        scratch_shapes=[pltpu.VMEM((seq + 2 * POOL_PAD, POOL_W), f32)],
        compiler_params=_cparams("arbitrary"),
        name="pool",
    )(p3, pool_w, pool_scale.reshape(1, POOL_W))


def _outproj_kernel(n_in, *refs):
    a_refs = refs[:n_in]
    w_refs = refs[n_in:2 * n_in]
    x_ref, mod_ref, o_ref = refs[2 * n_in:]
    y = None
    for a_ref, w_ref in zip(a_refs, w_refs):
        part = jnp.dot(a_ref[...].astype(bf16), w_ref[...], preferred_element_type=f32)
        y = part if y is None else y + part
    o_ref[...] = x_ref[...] + mod_ref[0][2:3] * y


def _outproj(acts, weights, x2, mod, seq, tm):
    t, d = x2.shape
    tps = seq // tm
    n_in = len(acts)
    tok = lambda i: (i, 0)
    in_specs = [pl.BlockSpec((tm, a.shape[1]), tok) for a in acts]
    in_specs += [pl.BlockSpec(w.shape, lambda i: (0, 0)) for w in weights]
    in_specs += [pl.BlockSpec((tm, d), tok), pl.BlockSpec((1, 6, d), lambda i: (i // tps, 0, 0))]
    return pl.pallas_call(
        functools.partial(_outproj_kernel, n_in),
        grid=(t // tm,),
        in_specs=in_specs,
        out_specs=pl.BlockSpec((tm, d), tok),
        out_shape=jax.ShapeDtypeStruct((t, d), f32),
        compiler_params=_cparams("arbitrary"),
        name="outproj",
    )(*acts, *[w.astype(bf16) for w in weights], x2, mod)


def _odd_inproj_kernel(x_ref, mod_ref, g_ref, w_ref, qg_ref, kg_ref, cos_ref, sin_ref,
                       q_ref, k_ref, v_ref):
    mod = mod_ref[0]
    h = _modulated_norm(x_ref[...], g_ref[...], mod[1:2], mod[0:1]).astype(bf16)
    z = jnp.dot(h, w_ref[...], preferred_element_type=f32)
    cosf = cos_ref[...]
    sinf = sin_ref[...]
    hd = ATT_HD

    def norm_rope(xh, gain):
        ms = jnp.mean(xh * xh, axis=-1, keepdims=True)
        xn = xh * lax.rsqrt(ms + NORM_EPS) * gain
        return xn * cosf + pltpu.roll(xn, hd // 2, axis=1) * sinf

    for hh in range(ATT_HEADS):
        qh = norm_rope(z[:, hh * hd:(hh + 1) * hd], qg_ref[...])
        q_ref[:, hh * hd:(hh + 1) * hd] = (qh * (hd ** -0.5)).astype(bf16)
    k0 = ATT_HEADS * hd
    for hh in range(ATT_KV_HEADS):
        kh = norm_rope(z[:, k0 + hh * hd:k0 + (hh + 1) * hd], kg_ref[...])
        k_ref[:, hh * hd:(hh + 1) * hd] = kh.astype(bf16)
    v0 = k0 + ATT_KV_HEADS * hd
    v_ref[...] = z[:, v0:v0 + ATT_KV_HEADS * hd].astype(bf16)


def _rope_tables(seq):
    rows = seq // GRID_W
    r, cidx = jnp.meshgrid(jnp.arange(rows), jnp.arange(GRID_W), indexing="ij")
    axis_dim = ATT_HD // 2
    freqs = ROPE_THETA ** (-jnp.arange(0, axis_dim, 2, dtype=f32) / axis_dim)
    ang = jnp.concatenate([r.reshape(-1, 1).astype(f32) * freqs,
                           cidx.reshape(-1, 1).astype(f32) * freqs], axis=-1)
    cos, sin = jnp.cos(ang), jnp.sin(ang)
    return jnp.concatenate([cos, cos], axis=-1), jnp.concatenate([-sin, sin], axis=-1)


def _odd_inproj(x2, mod, gain, w_in, qn_g, kn_g, seq, tm):
    t, d = x2.shape
    hd = ATT_HD
    n_qk = (ATT_HEADS + ATT_KV_HEADS) * hd
    n_all = n_qk + ATT_KV_HEADS * hd
    perm_head = jnp.concatenate([jnp.arange(0, hd, 2), jnp.arange(1, hd, 2)])
    perm = jnp.concatenate([(jnp.arange(n_qk) // hd) * hd + jnp.tile(perm_head, n_qk // hd),
                            jnp.arange(n_qk, n_all)])
    w = w_in[:, perm].astype(bf16)
    cosf, sinf = _rope_tables(seq)
    tps = seq // tm
    tok = lambda i: (i, 0)
    const = lambda i: (0, 0)
    return pl.pallas_call(
        _odd_inproj_kernel,
        grid=(t // tm,),
        in_specs=[pl.BlockSpec((tm, d), tok),
                  pl.BlockSpec((1, 6, d), lambda i: (i // tps, 0, 0)),
                  pl.BlockSpec((1, d), const),
                  pl.BlockSpec((d, n_all), const),
                  pl.BlockSpec((1, hd), const),
                  pl.BlockSpec((1, hd), const),
                  pl.BlockSpec((tm, hd), lambda i: (i % tps, 0)),
                  pl.BlockSpec((tm, hd), lambda i: (i % tps, 0))],
        out_specs=[pl.BlockSpec((tm, ATT_HEADS * hd), tok),
                   pl.BlockSpec((tm, ATT_KV_HEADS * hd), tok),
                   pl.BlockSpec((tm, ATT_KV_HEADS * hd), tok)],
        out_shape=[jax.ShapeDtypeStruct((t, ATT_HEADS * hd), bf16),
                   jax.ShapeDtypeStruct((t, ATT_KV_HEADS * hd), bf16),
                   jax.ShapeDtypeStruct((t, ATT_KV_HEADS * hd), bf16)],
        compiler_params=_cparams("arbitrary"),
        name="odd_inproj",
    )(x2, mod, gain.reshape(1, d), w, qn_g[perm_head].reshape(1, hd), kn_g[perm_head].reshape(1, hd),
      cosf, sinf)


def _attn_kernel(tk, q_ref, k_ref, v_ref, o_ref, m_scr, l_scr, acc_scr):
    S = k_ref.shape[1]
    tq = q_ref.shape[1]
    hd = ATT_HD
    q = jnp.concatenate([q_ref[0, :, g * hd:(g + 1) * hd] for g in range(ATT_GROUP)], axis=0)
    m_scr[...] = jnp.full(m_scr.shape, -jnp.inf, f32)
    l_scr[...] = jnp.zeros_like(l_scr)
    acc_scr[...] = jnp.zeros_like(acc_scr)

    def body(j, carry):
        start = pl.multiple_of(j * tk, tk)
        kc = k_ref[0, pl.ds(start, tk), :]
        vc = v_ref[0, pl.ds(start, tk), :]
        s = _nt_dot(q, kc)
        m_prev = m_scr[...]
        m_new = jnp.maximum(m_prev, jnp.max(s, axis=-1, keepdims=True))
        p = jnp.exp(s - m_new)
        alpha = jnp.exp(m_prev - m_new)
        l_scr[...] = alpha * l_scr[...] + jnp.sum(p, axis=-1, keepdims=True)
        acc_scr[...] = alpha * acc_scr[...] + jnp.dot(p.astype(bf16), vc, preferred_element_type=f32)
        m_scr[...] = m_new
        return carry

    lax.fori_loop(0, S // tk, body, 0)
    out = acc_scr[...] / l_scr[...]
    for g in range(ATT_GROUP):
        o_ref[0, :, g * hd:(g + 1) * hd] = out[g * tq:(g + 1) * tq].astype(o_ref.dtype)


def _attention(q, k, v, nb, seq, tq, tk):
    hd = ATT_HD
    gw = ATT_GROUP * hd
    q3 = q.reshape(nb, seq, ATT_HEADS * hd)
    k3 = k.reshape(nb, seq, ATT_KV_HEADS * hd)
    v3 = v.reshape(nb, seq, ATT_KV_HEADS * hd)
    rows = ATT_GROUP * tq
    return pl.pallas_call(
        functools.partial(_attn_kernel, tk),
        grid=(nb, ATT_KV_HEADS, seq // tq),
        in_specs=[pl.BlockSpec((1, tq, gw), lambda b, h, i: (b, i, h)),
                  pl.BlockSpec((1, seq, hd), lambda b, h, i: (b, 0, h)),
                  pl.BlockSpec((1, seq, hd), lambda b, h, i: (b, 0, h))],
        out_specs=pl.BlockSpec((1, tq, gw), lambda b, h, i: (b, i, h)),
        out_shape=jax.ShapeDtypeStruct((nb, seq, ATT_HEADS * hd), bf16),
        scratch_shapes=[pltpu.VMEM((rows, 1), f32), pltpu.VMEM((rows, 1), f32),
                        pltpu.VMEM((rows, hd), f32)],
        compiler_params=_cparams("arbitrary", "arbitrary", "arbitrary"),
        name="attention",
    )(q3, k3, v3)


def _topk_rows(s, k):
    rank = jnp.full(s.shape, NOT_SELECTED, f32)
    vals = []
    for i in range(k):
        m = jnp.max(s, axis=0, keepdims=True)
        hit = s == m
        rank = jnp.where(hit, float(i), rank)
        s = jnp.where(hit, -jnp.inf, s)
        vals.append(m)
    return jnp.concatenate(vals, axis=0), rank


def _kth_largest_rows(s, k):
    m = None
    for _ in range(k):
        m = jnp.max(s, axis=0, keepdims=True)
        s = jnp.where(s == m, -jnp.inf, s)
    return m


def _peer_route(h, wqt_ref, keys_ref, qt_scr, rank2_scr, c2_scr, cnt_scr, c1_scr):
    tm = h.shape[0]
    K = PEER_TOPK
    qt_scr[...] = _nt_dot(wqt_ref[...], h)
    keys = [keys_ref[p].astype(bf16) for p in range(2)]
    for tb in range(tm // LANES):
        ts = slice(tb * LANES, (tb + 1) * LANES)
        cnt_heads, c1_heads = [], []
        for hh in range(PEER_HEADS):
            svals, ranks, sraw = [], [], []
            for p in range(2):
                r0 = (hh * 2 + p) * PEER_HALF
                qb = qt_scr[r0:r0 + PEER_HALF, ts]
                ms = jnp.mean(qb * qb, axis=0, keepdims=True)
                qn = (qb * lax.rsqrt(ms + NORM_EPS)).astype(bf16)
                s = jnp.dot(keys[p], qn, preferred_element_type=f32)
                tv, rk = _topk_rows(s, K)
                svals.append(tv)
                ranks.append(rk)
                sraw.append(s)
            a, b = svals
            cand = jnp.concatenate(
                [a[0:1] + b] + [a[i:i + 1] + b[0:8] for i in range(1, 8)] + [a[8:16] + b[0:1]], axis=0)
            tau = _kth_largest_rows(cand, K)
            top = a[0:1] + b[0:1]
            zsum = jnp.zeros_like(tau)
            n_rows = []
            for i in range(K):
                pair = a[i:i + 1] + b
                sel = pair >= tau
                n_rows.append(jnp.sum(sel.astype(f32), axis=0, keepdims=True))
                zsum = zsum + jnp.sum(jnp.where(sel, jnp.exp(pair - top), 0.0), axis=0, keepdims=True)
            cnt = jnp.zeros_like(ranks[0])
            for i in range(K):
                cnt = jnp.where(ranks[0] == float(i), n_rows[i], cnt)
            rank2_scr[hh, :, ts] = ranks[1].astype(bf16)
            c2_scr[hh, :, ts] = jnp.exp(sraw[1] - b[0:1]).astype(bf16)
            cnt_heads.append(cnt)
            c1_heads.append(jnp.exp(sraw[0] - a[0:1]) / zsum)
        cnt_scr[:, :, ts] = pltpu.einshape("hit->iht", jnp.stack(cnt_heads))
        c1_scr[:, :, ts] = pltpu.einshape("hit->iht", jnp.stack(c1_heads))


def _peer_kernel(final, x_ref, mod_ref, g_ref, wqt_ref, keys_ref, u_ref, vt_ref, fg_ref, o_ref,
                 h_scr, qt_scr, rank2_scr, c2_scr, cnt_scr, c1_scr, wt_scr, acc_scr):
    e = pl.program_id(1)
    ne = pl.num_programs(1)
    tm = x_ref.shape[0]
    eb = u_ref.shape[0]
    mod = mod_ref[0]

    @pl.when(e == 0)
    def _():
        h = _modulated_norm(x_ref[...], g_ref[...], mod[4:5], mod[3:4]).astype(bf16)
        h_scr[...] = h
        _peer_route(h, wqt_ref, keys_ref, qt_scr, rank2_scr, c2_scr, cnt_scr, c1_scr)
        acc_scr[...] = jnp.zeros_like(acc_scr)

    at = _nt_dot(u_ref[...], h_scr[...])
    nsub = eb // PEER_NKEYS
    for ii in range(nsub):
        i = e * nsub + ii
        for tb in range(tm // LANES):
            ts = slice(tb * LANES, (tb + 1) * LANES)
            gate = jnp.zeros((PEER_NKEYS, LANES), bf16)
            for hh in range(PEER_HEADS):
                cnt = jnp.broadcast_to(cnt_scr[i, hh:hh + 1, ts], (PEER_NKEYS, LANES)).astype(bf16)
                c1 = jnp.broadcast_to(c1_scr[i, hh:hh + 1, ts], (PEER_NKEYS, LANES)).astype(bf16)
                sel = rank2_scr[hh, :, ts] < cnt
                gate = gate + jnp.where(sel, c2_scr[hh, :, ts], jnp.zeros_like(gate)) * c1
            pre = at[ii * PEER_NKEYS:(ii + 1) * PEER_NKEYS, ts]
            act = 0.5 * pre * (1.0 + lax.erf(pre * (2.0 ** -0.5)))
            wt_scr[ii * PEER_NKEYS:(ii + 1) * PEER_NKEYS, ts] = gate * act.astype(bf16)
    acc_scr[...] += jnp.dot(vt_ref[...], wt_scr[...], preferred_element_type=f32)

    @pl.when(e == ne - 1)
    def _():
        y = x_ref[...] + mod[5:6] * acc_scr[...].T
        if final:
            ms = jnp.mean(y * y, axis=-1, keepdims=True)
            y = y * lax.rsqrt(ms + NORM_EPS) * fg_ref[...]
        o_ref[...] = y


def _peer(x2, mod, gain, w_q, keys, u, v, final_g, seq, tm, eb, final):
    t, d = x2.shape
    nexp = u.shape[0]
    nq = w_q.shape[1]
    tps = seq // tm
    wqt = w_q.T.astype(bf16)
    ub = u.astype(bf16)
    vt = v.T.astype(bf16)
    tok = lambda i, e: (i, 0)
    const = lambda i, e: (0, 0)
    nk = PEER_NKEYS
    return pl.pallas_call(
        functools.partial(_peer_kernel, final),
        grid=(t // tm, nexp // eb),
        in_specs=[pl.BlockSpec((tm, d), tok),
                  pl.BlockSpec((1, 6, d), lambda i, e: (i // tps, 0, 0)),
                  pl.BlockSpec((1, d), const),
                  pl.BlockSpec((nq, d), const),
                  pl.BlockSpec(keys.shape, lambda i, e: (0, 0, 0)),
                  pl.BlockSpec((eb, d), lambda i, e: (e, 0)),
                  pl.BlockSpec((d, eb), lambda i, e: (0, e)),
                  pl.BlockSpec((1, d), const)],
        out_specs=pl.BlockSpec((tm, d), tok),
        out_shape=jax.ShapeDtypeStruct((t, d), f32),
        scratch_shapes=[pltpu.VMEM((tm, d), bf16),
                        pltpu.VMEM((nq, tm), f32),
                        pltpu.VMEM((PEER_HEADS, nk, tm), bf16),
                        pltpu.VMEM((PEER_HEADS, nk, tm), bf16),
                        pltpu.VMEM((nk, PEER_HEADS, tm), f32),
                        pltpu.VMEM((nk, PEER_HEADS, tm), f32),
                        pltpu.VMEM((eb, tm), bf16),
                        pltpu.VMEM((d, tm), f32)],
        compiler_params=_cparams("arbitrary", "arbitrary"),
        name="peer",
    )(x2, mod, gain.reshape(1, d), wqt, keys, ub, vt, final_g.reshape(1, d))


def _pick_tile(seq, want):
    tm = min(seq, want)
    assert seq % tm == 0
    return tm


def kernel(x, c, ada_w, ada_b, norm_mix_g, norm_ffn_g, ev_w_in, ev_b_in, ev_conv_w, ev_conv_b, ev_mnorm_g, ev_pool_w, ev_pool_scale, ev_w_out, od_w_in, od_qnorm_g, od_knorm_g, od_w_out, peer_w_q, peer_keys, peer_u, peer_v, final_g):
    nb, seq, d = x.shape
    depth = ada_w.shape[0]
    t = nb * seq
    tm = _pick_tile(seq, 512)
    mods = _ada_mod(c, ada_w, ada_b).reshape(depth, nb, 6, d)
    x2 = x.reshape(t, d)
    for i in range(depth):
        mod = mods[i]
        j = i // 2
        if i % 2 == 0:
            q, v, o, p, kt, gt = _even_inproj(x2, mod, norm_mix_g[i], ev_w_in[j], ev_b_in[j], seq, tm)
            hm = _mlstm(q, kt, v, o, gt, ev_conv_w[j], ev_conv_b[j], ev_mnorm_g[j], nb, seq)
            yp = _pool(p, ev_pool_w[j], ev_pool_scale[j], nb, seq)
            x2 = _outproj([hm.reshape(t, MLSTM_W), yp.reshape(t, POOL_W)],
                          [ev_w_out[j][:MLSTM_W], ev_w_out[j][MLSTM_W:]], x2, mod, seq, tm)
        else:
            q, k, v = _odd_inproj(x2, mod, norm_mix_g[i], od_w_in[j], od_qnorm_g[j], od_knorm_g[j], seq, tm)
            att = _attention(q, k, v, nb, seq, _pick_tile(seq, 256), _pick_tile(seq, 512))
            x2 = _outproj([att.reshape(t, ATT_HEADS * ATT_HD)], [od_w_out[j]], x2, mod, seq, tm)
        x2 = _peer(x2, mod, norm_ffn_g[i], peer_w_q[i], peer_keys[i], peer_u[i], peer_v[i], final_g,
                   seq, tm, 512, final=(i == depth - 1))
    return x2.reshape(nb, seq, d)
```
